```python
import jax, jax.numpy as jnp
from jax import lax
import numpy as np

D_MODEL = 2048
BATCH = 4
SEQ = 2048
DEPTH = 1

CHUNK = 64
Q_BLOCK = 128
CONV_WIDTH = D_MODEL // 2
CONV_GROUPS = 8
CONV_K = 3
V_DIM = 128
N_HEADS = (D_MODEL - CONV_WIDTH) // V_DIM
QK_NOPE = 128
QK_ROPE = 64
Q_RANK = 768
KV_RANK = 512
ATTN_WIDTH = N_HEADS * V_DIM
MIX_WIDTH = CONV_WIDTH + ATTN_WIDTH
IN_WIDTH = 3 * CONV_WIDTH + Q_RANK + KV_RANK + QK_ROPE
D_FF = 4 * D_MODEL
ROPE_THETA = 10000.0
EPS = 1e-6
NEG_INF = -1e30

kernel_name = "hybrid_conv_mla_sandwich_block"


def rms_norm(x, g):
    x32 = x.astype(jnp.float32)
    y = x32 * lax.rsqrt(jnp.mean(x32 * x32, axis=-1, keepdims=True) + EPS)
    return (y * g.astype(jnp.float32)).astype(x.dtype)


def group_rms_norm(x, g, n_groups):
    b, s, w = x.shape
    y = rms_norm(x.reshape(b, s, n_groups, w // n_groups), g.reshape(n_groups, w // n_groups))
    return y.reshape(b, s, w)


def apply_rope(x, cos, sin):
    x32 = x.astype(jnp.float32)
    x1, x2 = jnp.split(x32, 2, axis=-1)
    return jnp.concatenate([x1 * cos - x2 * sin, x2 * cos + x1 * sin], axis=-1).astype(x.dtype)


def short_conv_causal(u, w):
    rhs = w[:, None, :].astype(u.dtype)
    return lax.conv_general_dilated(
        u, rhs, window_strides=(1,), padding=[(CONV_K - 1, 0)],
        dimension_numbers=("NWC", "WIO", "NWC"), feature_group_count=u.shape[-1])


def chunk_causal_mla(q_nope, q_rope, k_nope, k_rope, v):
    s_len = q_nope.shape[1]
    scale = (QK_NOPE + QK_ROPE) ** -0.5
    chunk_id = jnp.arange(s_len) // CHUNK
    outs = []
    for start in range(0, s_len, Q_BLOCK):
        end = start + Q_BLOCK
        s = (jnp.einsum("bqhd,bkhd->bhqk", q_nope[:, start:end], k_nope[:, :end],
                        preferred_element_type=jnp.float32)
             + jnp.einsum("bqhr,bkr->bhqk", q_rope[:, start:end], k_rope[:, :end],
                          preferred_element_type=jnp.float32)) * scale
        visible = chunk_id[None, :end] <= chunk_id[start:end, None]
        s = jnp.where(visible, s, NEG_INF)
        p = jax.nn.softmax(s, axis=-1).astype(v.dtype)
        outs.append(jnp.einsum("bhqk,bkhd->bqhd", p, v[:, :end]))
    return jnp.concatenate(outs, axis=1)


def hybrid_mixer(h, w_in, conv_w, q_norm_g, w_uq, kv_norm_g, w_ukv, conv_out_g, attn_out_g, w_o):
    b, s, _ = h.shape
    proj = h @ w_in
    cuts = np.cumsum([CONV_WIDTH, CONV_WIDTH, CONV_WIDTH, Q_RANK, KV_RANK]).tolist()
    u, gate_b, gate_c, c_q, c_kv, k_rope_raw = jnp.split(proj, cuts, axis=-1)

    y_conv = gate_b * short_conv_causal(gate_c * u, conv_w)
    y_conv = group_rms_norm(y_conv, conv_out_g, CONV_GROUPS)

    q = (rms_norm(c_q, q_norm_g) @ w_uq).reshape(b, s, N_HEADS, QK_NOPE + QK_ROPE)
    q_nope, q_rope = q[..., :QK_NOPE], q[..., QK_NOPE:]
    kv = (rms_norm(c_kv, kv_norm_g) @ w_ukv).reshape(b, s, N_HEADS, QK_NOPE + V_DIM)
    k_nope, v = kv[..., :QK_NOPE], kv[..., QK_NOPE:]
    pos = jnp.arange(s, dtype=jnp.float32)
    inv_freq = jnp.power(ROPE_THETA, -jnp.arange(0, QK_ROPE, 2, dtype=jnp.float32) / QK_ROPE)
    ang = pos[:, None] * inv_freq[None, :]
    cos, sin = jnp.cos(ang), jnp.sin(ang)
    q_rope = apply_rope(q_rope, cos[None, :, None, :], sin[None, :, None, :])
    k_rope = apply_rope(k_rope_raw, cos[None], sin[None])
    o = chunk_causal_mla(q_nope, q_rope, k_nope, k_rope, v)
    y_attn = group_rms_norm(o.reshape(b, s, ATTN_WIDTH), attn_out_g, N_HEADS)

    return jnp.concatenate([y_conv, y_attn], axis=-1) @ w_o


def setup_inputs(seed: int = 0) -> dict:
    key = jax.random.key(seed)
    ks = jax.random.split(key, 17)

    def w(k, shape, fan_in):
        return jax.random.normal(k, shape, jnp.float32) * (fan_in ** -0.5)

    def gain(k, n):
        return 1.0 + 0.05 * jax.random.normal(k, (DEPTH, n), jnp.float32)

    return {
        "x": jax.random.normal(ks[0], (BATCH, SEQ, D_MODEL), jnp.float32),
        "pre_mix_g": gain(ks[1], D_MODEL),
        "w_in": w(ks[2], (DEPTH, D_MODEL, IN_WIDTH), D_MODEL),
        "conv_w": w(ks[3], (DEPTH, CONV_K, CONV_WIDTH), CONV_K),
        "q_norm_g": gain(ks[4], Q_RANK),
        "w_uq": w(ks[5], (DEPTH, Q_RANK, N_HEADS * (QK_NOPE + QK_ROPE)), Q_RANK),
        "kv_norm_g": gain(ks[6], KV_RANK),
        "w_ukv": w(ks[7], (DEPTH, KV_RANK, N_HEADS * (QK_NOPE + V_DIM)), KV_RANK),
        "conv_out_g": gain(ks[8], CONV_WIDTH),
        "attn_out_g": gain(ks[9], ATTN_WIDTH),
        "w_o": w(ks[10], (DEPTH, MIX_WIDTH, D_MODEL), MIX_WIDTH),
        "post_mix_g": gain(ks[11], D_MODEL),
        "pre_mlp_g": gain(ks[12], D_MODEL),
        "w_up": w(ks[13], (DEPTH, D_MODEL, D_FF), D_MODEL),
        "w_down": w(ks[14], (DEPTH, D_FF, D_MODEL), D_FF),
        "post_mlp_g": gain(ks[15], D_MODEL),
    }


def reference(x, pre_mix_g, w_in, conv_w, q_norm_g, w_uq, kv_norm_g, w_ukv, conv_out_g,
              attn_out_g, w_o, post_mix_g, pre_mlp_g, w_up, w_down, post_mlp_g):
    for l in range(DEPTH):
        h = rms_norm(x, pre_mix_g[l])
        y = hybrid_mixer(h, w_in[l], conv_w[l], q_norm_g[l], w_uq[l], kv_norm_g[l], w_ukv[l],
                         conv_out_g[l], attn_out_g[l], w_o[l])
        x = x + rms_norm(y, post_mix_g[l])
        h = rms_norm(x, pre_mlp_g[l])
        m = jnp.square(jax.nn.relu(h @ w_up[l])) @ w_down[l]
        x = x + rms_norm(m, post_mlp_g[l])
    return x
```

```python
import functools

import numpy as np
import jax
import jax.numpy as jnp
from jax import lax
from jax.experimental import pallas as pl
from jax.experimental.pallas import tpu as pltpu

D_MODEL = 2048
BATCH = 4
SEQ = 2048
TOKENS = BATCH * SEQ
CHUNK = 64
CONV_WIDTH = D_MODEL // 2
CONV_GROUPS = 8
CONV_K = 3
V_DIM = 128
N_HEADS = 8
QK_NOPE = 128
QK_ROPE = 64
Q_RANK = 768
KV_RANK = 512
ATTN_WIDTH = N_HEADS * V_DIM
D_FF = 4 * D_MODEL
ROPE_THETA = 10000.0
EPS = 1e-6
NEG_INF = -1e30
SM_SCALE = (QK_NOPE + QK_ROPE) ** -0.5

LANES = 128
SUBLANES = 8
QK_PAD = 2 * LANES
V7X_VMEM_LIMIT_BYTES = 60000 * 1024

_U0, _B0, _C0 = 0, CONV_WIDTH, 2 * CONV_WIDTH
_CQ0 = 3 * CONV_WIDTH
_CKV0 = _CQ0 + Q_RANK
_KR0 = _CKV0 + KV_RANK
IN_PAD = _KR0 + LANES

BF16 = jnp.bfloat16
F32 = jnp.float32


def _rms(x, g):
    ms = jnp.mean(x * x, axis=-1, keepdims=True)
    return x * lax.rsqrt(ms + EPS) * g


def _dot(a, b):
    return jnp.dot(a, b, preferred_element_type=F32)


def _rope_pairs(v, cos, sin_lo, sin_hi):
    return v * cos + pltpu.roll(v, LANES - 32, 1) * sin_lo + pltpu.roll(v, 32, 1) * sin_hi


def _mix_in_kernel(x_ref, gpre_ref, win_ref, convw_ref, qg_ref, wuq_ref, kvg_ref, wukv_ref,
                   convg_ref, cos_ref, slo_ref, shi_ref,
                   yconv_ref, q_ref, k_ref, v_ref, h_scr, cu_scr, *, tm, tiles_per_seq):
    i = pl.program_id(0)

    @pl.when(i % tiles_per_seq == 0)
    def _():
        cu_scr[0:SUBLANES, :] = jnp.zeros((SUBLANES, CONV_WIDTH), F32)

    h_scr[...] = _rms(x_ref[...], gpre_ref[...]).astype(BF16)
    h = h_scr[...]

    cw = 2 * LANES
    for c in range(0, CONV_WIDTH, cw):
        u = _dot(h, win_ref[:, _U0 + c:_U0 + c + cw])
        gb = _dot(h, win_ref[:, _B0 + c:_B0 + c + cw])
        gc = _dot(h, win_ref[:, _C0 + c:_C0 + c + cw])
        cu_scr[SUBLANES:SUBLANES + tm, c:c + cw] = gc * u
        cu0 = cu_scr[SUBLANES:SUBLANES + tm, c:c + cw]
        cu1 = cu_scr[SUBLANES - 1:SUBLANES - 1 + tm, c:c + cw]
        cu2 = cu_scr[SUBLANES - 2:SUBLANES - 2 + tm, c:c + cw]
        w = convw_ref[:, c:c + cw]
        y = gb * (w[0:1, :] * cu2 + w[1:2, :] * cu1 + w[2:3, :] * cu0)
        for g in range(0, cw, LANES):
            yconv_ref[:, c + g:c + g + LANES] = _rms(
                y[:, g:g + LANES], convg_ref[:, c + g:c + g + LANES]).astype(BF16)
    cu_scr[SUBLANES - 2:SUBLANES, :] = cu_scr[SUBLANES + tm - 2:SUBLANES + tm, :]

    cos, slo, shi = cos_ref[...], slo_ref[...], shi_ref[...]
    lane = lax.broadcasted_iota(jnp.int32, (tm, LANES), 1)

    cqn = _rms(_dot(h, win_ref[:, _CQ0:_CQ0 + Q_RANK]), qg_ref[...]).astype(BF16)
    qn = _dot(cqn, wuq_ref[:, 0:N_HEADS * QK_NOPE]) * SM_SCALE
    qr = _dot(cqn, wuq_ref[:, N_HEADS * QK_NOPE:])
    for hd in range(N_HEADS):
        q_ref[:, hd * QK_PAD:hd * QK_PAD + LANES] = qn[:, hd * LANES:(hd + 1) * LANES].astype(BF16)
    for pr in range(N_HEADS // 2):
        roped = _rope_pairs(qr[:, pr * LANES:(pr + 1) * LANES], cos, slo, shi) * SM_SCALE
        lo = jnp.where(lane < QK_ROPE, roped, 0.0).astype(BF16)
        hi = jnp.where(lane >= QK_ROPE, roped, 0.0).astype(BF16)
        q_ref[:, (2 * pr) * QK_PAD + LANES:(2 * pr + 1) * QK_PAD] = lo
        q_ref[:, (2 * pr + 1) * QK_PAD + LANES:(2 * pr + 2) * QK_PAD] = hi

    ckvn = _rms(_dot(h, win_ref[:, _CKV0:_CKV0 + KV_RANK]), kvg_ref[...]).astype(BF16)
    kn = _dot(ckvn, wukv_ref[:, 0:N_HEADS * QK_NOPE])
    v_ref[...] = _dot(ckvn, wukv_ref[:, N_HEADS * QK_NOPE:]).astype(BF16)
    kr = _rope_pairs(_dot(h, win_ref[:, _KR0:_KR0 + LANES]), cos, slo, shi).astype(BF16)
    for hd in range(N_HEADS):
        k_ref[:, hd * QK_PAD:hd * QK_PAD + LANES] = kn[:, hd * LANES:(hd + 1) * LANES].astype(BF16)
        k_ref[:, hd * QK_PAD + LANES:(hd + 1) * QK_PAD] = kr


def _const_spec(shape):
    return pl.BlockSpec(shape, lambda *_: (0,) * len(shape), pipeline_mode=pl.Buffered(1))


def _mix_in_call(x, gpre, win, convw, qg, wuq, kvg, wukv, convg, cos, slo, shi, *, tm=512):
    tiles_per_seq = SEQ // tm
    row = lambda w: pl.BlockSpec((tm, w), lambda i: (i, 0))
    pos = lambda: pl.BlockSpec((tm, LANES), lambda i: (i % tiles_per_seq, 0))
    return pl.pallas_call(
        functools.partial(_mix_in_kernel, tm=tm, tiles_per_seq=tiles_per_seq),
        grid=(TOKENS // tm,),
        in_specs=[row(D_MODEL), _const_spec((1, D_MODEL)), _const_spec((D_MODEL, IN_PAD)),
                  _const_spec((SUBLANES, CONV_WIDTH)), _const_spec((1, Q_RANK)),
                  _const_spec(wuq.shape), _const_spec((1, KV_RANK)), _const_spec(wukv.shape),
                  _const_spec((1, CONV_WIDTH)), pos(), pos(), pos()],
        out_specs=[row(CONV_WIDTH), row(N_HEADS * QK_PAD), row(N_HEADS * QK_PAD), row(ATTN_WIDTH)],
        out_shape=[jax.ShapeDtypeStruct((TOKENS, CONV_WIDTH), BF16),
                   jax.ShapeDtypeStruct((TOKENS, N_HEADS * QK_PAD), BF16),
                   jax.ShapeDtypeStruct((TOKENS, N_HEADS * QK_PAD), BF16),
                   jax.ShapeDtypeStruct((TOKENS, ATTN_WIDTH), BF16)],
        scratch_shapes=[pltpu.VMEM((tm, D_MODEL), BF16),
                        pltpu.VMEM((tm + SUBLANES, CONV_WIDTH), F32)],
        compiler_params=pltpu.CompilerParams(
            dimension_semantics=("arbitrary",), vmem_limit_bytes=V7X_VMEM_LIMIT_BYTES),
        name="mix_in",
    )(x, gpre, win, convw, qg, wuq, kvg, wukv, convg, cos, slo, shi)


def _attn_kernel(q_ref, k_ref, v_ref, g_ref, o_ref, *, tq, tk):
    qi = pl.program_id(2)
    q = q_ref[...]
    row_chunk = (qi * tq + lax.broadcasted_iota(jnp.int32, (tq, tk), 0)) // CHUNK
    col_in_tile = lax.broadcasted_iota(jnp.int32, (tq, tk), 1)

    def body(j, carry):
        m, l, acc = carry
        k = k_ref[pl.ds(pl.multiple_of(j * tk, tk), tk), :]
        v = v_ref[pl.ds(pl.multiple_of(j * tk, tk), tk), :]
        s = lax.dot_general(q, k, (((1,), (1,)), ((), ())), preferred_element_type=F32)
        visible = (j * tk + col_in_tile) // CHUNK <= row_chunk
        s = jnp.where(visible, s, NEG_INF)
        m_new = jnp.maximum(m, jnp.max(s, axis=-1, keepdims=True))
        alpha = jnp.exp(m - m_new)
        p = jnp.exp(s - m_new)
        l = alpha * l + jnp.sum(p, axis=-1, keepdims=True)
        acc = alpha * acc + _dot(p.astype(BF16), v)
        return m_new, l, acc

    init = (jnp.full((tq, 1), NEG_INF, F32), jnp.zeros((tq, 1), F32), jnp.zeros((tq, V_DIM), F32))
    _, l, acc = lax.fori_loop(0, (qi * tq) // tk + tq // tk, body, init)
    o_ref[...] = _rms(acc / l, g_ref[...]).astype(BF16)


def _attn_call(q, k, v, g, *, tq=256, tk=256):
    nq = SEQ // tq
    return pl.pallas_call(
        functools.partial(_attn_kernel, tq=tq, tk=tk),
        grid=(BATCH, N_HEADS, nq),
        in_specs=[pl.BlockSpec((tq, QK_PAD), lambda b, h, i: (b * nq + i, h)),
                  pl.BlockSpec((SEQ, QK_PAD), lambda b, h, i: (b, h)),
                  pl.BlockSpec((SEQ, V_DIM), lambda b, h, i: (b, h)),
                  pl.BlockSpec((1, V_DIM), lambda b, h, i: (0, h))],
        out_specs=pl.BlockSpec((tq, V_DIM), lambda b, h, i: (b * nq + i, h)),
        out_shape=jax.ShapeDtypeStruct((TOKENS, ATTN_WIDTH), BF16),
        compiler_params=pltpu.CompilerParams(
            dimension_semantics=("arbitrary", "arbitrary", "arbitrary"),
            vmem_limit_bytes=V7X_VMEM_LIMIT_BYTES),
        name="attn",
    )(q, k, v, g)


def _mix_out_kernel(x_ref, yc_ref, ya_ref, wo_ref, gpost_ref, gmlp_ref, x1_ref, h2_ref):
    y = _dot(yc_ref[...], wo_ref[0:CONV_WIDTH, :]) + _dot(ya_ref[...], wo_ref[CONV_WIDTH:, :])
    x1 = x_ref[...] + _rms(y, gpost_ref[...])
    x1_ref[...] = x1
    h2_ref[...] = _rms(x1, gmlp_ref[...]).astype(BF16)


def _mix_out_call(x, yc, ya, wo, gpost, gmlp, *, tm=512):
    row = lambda w: pl.BlockSpec((tm, w), lambda i: (i, 0))
    return pl.pallas_call(
        _mix_out_kernel,
        grid=(TOKENS // tm,),
        in_specs=[row(D_MODEL), row(CONV_WIDTH), row(ATTN_WIDTH), _const_spec(wo.shape),
                  _const_spec((1, D_MODEL)), _const_spec((1, D_MODEL))],
        out_specs=[row(D_MODEL), row(D_MODEL)],
        out_shape=[jax.ShapeDtypeStruct((TOKENS, D_MODEL), F32),
                   jax.ShapeDtypeStruct((TOKENS, D_MODEL), BF16)],
        compiler_params=pltpu.CompilerParams(
            dimension_semantics=("arbitrary",), vmem_limit_bytes=V7X_VMEM_LIMIT_BYTES),
        name="mix_out",
    )(x, yc, ya, wo, gpost, gmlp)


def _mlp_kernel(h2_ref, wup_ref, wdn_ref, x1_ref, g_ref, o_ref, acc_ref):
    j = pl.program_id(1)
    a = jnp.maximum(_dot(h2_ref[...], wup_ref[...]), 0.0)
    part = _dot((a * a).astype(BF16), wdn_ref[...])

    @pl.when(j == 0)
    def _():
        acc_ref[...] = part

    @pl.when(j > 0)
    def _():
        acc_ref[...] += part

    @pl.when(j == pl.num_programs(1) - 1)
    def _():
        o_ref[...] = x1_ref[...] + _rms(acc_ref[...], g_ref[...])


def _mlp_call(h2, wup, wdn, x1, g, *, tm=512, tf=1024):
    return pl.pallas_call(
        _mlp_kernel,
        grid=(TOKENS // tm, D_FF // tf),
        in_specs=[pl.BlockSpec((tm, D_MODEL), lambda i, j: (i, 0)),
                  pl.BlockSpec((D_MODEL, tf), lambda i, j: (0, j)),
                  pl.BlockSpec((tf, D_MODEL), lambda i, j: (j, 0)),
                  pl.BlockSpec((tm, D_MODEL), lambda i, j: (i, 0)),
                  _const_spec((1, D_MODEL))],
        out_specs=pl.BlockSpec((tm, D_MODEL), lambda i, j: (i, 0)),
        out_shape=jax.ShapeDtypeStruct((TOKENS, D_MODEL), F32),
        scratch_shapes=[pltpu.VMEM((tm, D_MODEL), F32)],
        compiler_params=pltpu.CompilerParams(
            dimension_semantics=("arbitrary", "arbitrary"), vmem_limit_bytes=V7X_VMEM_LIMIT_BYTES),
        name="mlp",
    )(h2, wup, wdn, x1, g)


def _rope_tables():
    pos = np.arange(SEQ, dtype=np.float32)[:, None]
    inv_freq = np.power(np.float32(ROPE_THETA),
                        -np.arange(0, QK_ROPE, 2, dtype=np.float32) / np.float32(QK_ROPE))
    ang = (pos * inv_freq[None, :]).astype(np.float32)
    cos, sin = np.cos(ang).astype(np.float32), np.sin(ang).astype(np.float32)
    zero = np.zeros_like(sin)
    cos4 = np.concatenate([cos, cos, cos, cos], axis=1)
    sin_lo = np.concatenate([-sin, zero, -sin, zero], axis=1)
    sin_hi = np.concatenate([zero, sin, zero, sin], axis=1)
    return jnp.asarray(cos4), jnp.asarray(sin_lo), jnp.asarray(sin_hi)


def kernel(x, pre_mix_g, w_in, conv_w, q_norm_g, w_uq, kv_norm_g, w_ukv, conv_out_g, attn_out_g,
           w_o, post_mix_g, pre_mlp_g, w_up, w_down, post_mlp_g):
    cos4, sin_lo, sin_hi = _rope_tables()
    x2d = x.reshape(TOKENS, D_MODEL)
    for l in range(w_in.shape[0]):
        win = jnp.concatenate([w_in[l], w_in[l][:, _KR0:_KR0 + QK_ROPE]], axis=1).astype(BF16)
        wq3 = w_uq[l].reshape(Q_RANK, N_HEADS, QK_NOPE + QK_ROPE)
        wuq = jnp.concatenate([wq3[:, :, :QK_NOPE].reshape(Q_RANK, N_HEADS * QK_NOPE),
                               wq3[:, :, QK_NOPE:].reshape(Q_RANK, N_HEADS * QK_ROPE)],
                              axis=1).astype(BF16)
        wkv4 = w_ukv[l].reshape(KV_RANK, N_HEADS, 2, V_DIM)
        wukv = jnp.transpose(wkv4, (0, 2, 1, 3)).reshape(KV_RANK, 2 * N_HEADS * V_DIM).astype(BF16)
        convw = jnp.pad(conv_w[l], ((0, SUBLANES - CONV_K), (0, 0)))
        row = lambda g: g[l].reshape(1, -1)

        yc, q, k, v = _mix_in_call(x2d, row(pre_mix_g), win, convw, row(q_norm_g), wuq,
                                   row(kv_norm_g), wukv, row(conv_out_g), cos4, sin_lo, sin_hi)
        ya = _attn_call(q, k, v, row(attn_out_g))
        x1, h2 = _mix_out_call(x2d, yc, ya, w_o[l].astype(BF16), row(post_mix_g), row(pre_mlp_g))
        x2d = _mlp_call(h2, w_up[l].astype(BF16), w_down[l].astype(BF16), x1, row(post_mlp_g))
    return x2d.reshape(BATCH, SEQ, D_MODEL)
```

```python
import functools

import numpy as np
import jax
import jax.numpy as jnp
from jax import lax
from jax.experimental import pallas as pl
from jax.experimental.pallas import tpu as pltpu

D_MODEL = 2048
BATCH = 4
SEQ = 2048
TOKENS = BATCH * SEQ
CHUNK = 64
CONV_WIDTH = D_MODEL // 2
CONV_GROUPS = 8
CONV_K = 3
V_DIM = 128
N_HEADS = 8
QK_NOPE = 128
QK_ROPE = 64
Q_RANK = 768
KV_RANK = 512
ATTN_WIDTH = N_HEADS * V_DIM
D_FF = 4 * D_MODEL
ROPE_THETA = 10000.0
EPS = 1e-6
NEG_INF = -1e30
SM_SCALE = (QK_NOPE + QK_ROPE) ** -0.5

LANES = 128
SUBLANES = 8
QK_PAD = 2 * LANES
V7X_VMEM_LIMIT_BYTES = 60000 * 1024

_U0, _B0, _C0 = 0, CONV_WIDTH, 2 * CONV_WIDTH
_CQ0 = 3 * CONV_WIDTH
_CKV0 = _CQ0 + Q_RANK
_KR0 = _CKV0 + KV_RANK
IN_PAD = _KR0 + LANES

BF16 = jnp.bfloat16
F32 = jnp.float32


def _rms(x, g):
    ms = jnp.mean(x * x, axis=-1, keepdims=True)
    return x * lax.rsqrt(ms + EPS) * g


def _dot(a, b):
    return jnp.dot(a, b, preferred_element_type=F32)


def _rope_pairs(v, cos, sin_lo, sin_hi):
    return v * cos + pltpu.roll(v, LANES - 32, 1) * sin_lo + pltpu.roll(v, 32, 1) * sin_hi


def _mix_in_kernel(x_ref, gpre_ref, win_ref, convw_ref, qg_ref, wuq_ref, kvg_ref, wukv_ref,
                   convg_ref, cos_ref, slo_ref, shi_ref,
                   yconv_ref, q_ref, k_ref, v_ref, h_scr, cu_scr, *, tm, tiles_per_seq):
    i = pl.program_id(0)

    @pl.when(i % tiles_per_seq == 0)
    def _():
        cu_scr[0:SUBLANES, :] = jnp.zeros((SUBLANES, CONV_WIDTH), F32)

    h_scr[...] = _rms(x_ref[...], gpre_ref[...]).astype(BF16)
    h = h_scr[...]

    cw = 2 * LANES
    for c in range(0, CONV_WIDTH, cw):
        u = _dot(h, win_ref[:, _U0 + c:_U0 + c + cw])
        gb = _dot(h, win_ref[:, _B0 + c:_B0 + c + cw])
        gc = _dot(h, win_ref[:, _C0 + c:_C0 + c + cw])
        cu_scr[SUBLANES:SUBLANES + tm, c:c + cw] = gc * u
        cu0 = cu_scr[SUBLANES:SUBLANES + tm, c:c + cw]
        cu1 = cu_scr[SUBLANES - 1:SUBLANES - 1 + tm, c:c + cw]
        cu2 = cu_scr[SUBLANES - 2:SUBLANES - 2 + tm, c:c + cw]
        w = convw_ref[:, c:c + cw]
        y = gb * (w[0:1, :] * cu2 + w[1:2, :] * cu1 + w[2:3, :] * cu0)
        for g in range(0, cw, LANES):
            yconv_ref[:, c + g:c + g + LANES] = _rms(
                y[:, g:g + LANES], convg_ref[:, c + g:c + g + LANES]).astype(BF16)
    cu_scr[SUBLANES - 2:SUBLANES, :] = cu_scr[SUBLANES + tm - 2:SUBLANES + tm, :]

    cos, slo, shi = cos_ref[...], slo_ref[...], shi_ref[...]
    lane = lax.broadcasted_iota(jnp.int32, (tm, LANES), 1)

    cqn = _rms(_dot(h, win_ref[:, _CQ0:_CQ0 + Q_RANK]), qg_ref[...]).astype(BF16)
    qn = _dot(cqn, wuq_ref[:, 0:N_HEADS * QK_NOPE]) * SM_SCALE
    qr = _dot(cqn, wuq_ref[:, N_HEADS * QK_NOPE:])
    for hd in range(N_HEADS):
        q_ref[:, hd * QK_PAD:hd * QK_PAD + LANES] = qn[:, hd * LANES:(hd + 1) * LANES].astype(BF16)
    for pr in range(N_HEADS // 2):
        roped = _rope_pairs(qr[:, pr * LANES:(pr + 1) * LANES], cos, slo, shi) * SM_SCALE
        lo = jnp.where(lane < QK_ROPE, roped, 0.0).astype(BF16)
        hi = jnp.where(lane >= QK_ROPE, roped, 0.0).astype(BF16)
        q_ref[:, (2 * pr) * QK_PAD + LANES:(2 * pr + 1) * QK_PAD] = lo
        q_ref[:, (2 * pr + 1) * QK_PAD + LANES:(2 * pr + 2) * QK_PAD] = hi

    ckvn = _rms(_dot(h, win_ref[:, _CKV0:_CKV0 + KV_RANK]), kvg_ref[...]).astype(BF16)
    kn = _dot(ckvn, wukv_ref[:, 0:N_HEADS * QK_NOPE])
    v_ref[...] = _dot(ckvn, wukv_ref[:, N_HEADS * QK_NOPE:]).astype(BF16)
    kr = _rope_pairs(_dot(h, win_ref[:, _KR0:_KR0 + LANES]), cos, slo, shi).astype(BF16)
    for hd in range(N_HEADS):
        k_ref[:, hd * QK_PAD:hd * QK_PAD + LANES] = kn[:, hd * LANES:(hd + 1) * LANES].astype(BF16)
        k_ref[:, hd * QK_PAD + LANES:(hd + 1) * QK_PAD] = kr


def _const_spec(shape):
    return pl.BlockSpec(shape, lambda *_: (0,) * len(shape), pipeline_mode=pl.Buffered(1))


def _mix_in_call(x, gpre, win, convw, qg, wuq, kvg, wukv, convg, cos, slo, shi, *, tm=512):
    tiles_per_seq = SEQ // tm
    row = lambda w: pl.BlockSpec((tm, w), lambda i: (i, 0))
    pos = lambda: pl.BlockSpec((tm, LANES), lambda i: (i % tiles_per_seq, 0))
    return pl.pallas_call(
        functools.partial(_mix_in_kernel, tm=tm, tiles_per_seq=tiles_per_seq),
        grid=(TOKENS // tm,),
        in_specs=[row(D_MODEL), _const_spec((1, D_MODEL)), _const_spec((D_MODEL, IN_PAD)),
                  _const_spec((SUBLANES, CONV_WIDTH)), _const_spec((1, Q_RANK)),
                  _const_spec(wuq.shape), _const_spec((1, KV_RANK)), _const_spec(wukv.shape),
                  _const_spec((1, CONV_WIDTH)), pos(), pos(), pos()],
        out_specs=[row(CONV_WIDTH), row(N_HEADS * QK_PAD), row(N_HEADS * QK_PAD), row(ATTN_WIDTH)],
        out_shape=[jax.ShapeDtypeStruct((TOKENS, CONV_WIDTH), BF16),
                   jax.ShapeDtypeStruct((TOKENS, N_HEADS * QK_PAD), BF16),
                   jax.ShapeDtypeStruct((TOKENS, N_HEADS * QK_PAD), BF16),
                   jax.ShapeDtypeStruct((TOKENS, ATTN_WIDTH), BF16)],
        scratch_shapes=[pltpu.VMEM((tm, D_MODEL), BF16),
                        pltpu.VMEM((tm + SUBLANES, CONV_WIDTH), F32)],
        compiler_params=pltpu.CompilerParams(
            dimension_semantics=("arbitrary",), vmem_limit_bytes=V7X_VMEM_LIMIT_BYTES),
        name="mix_in",
    )(x, gpre, win, convw, qg, wuq, kvg, wukv, convg, cos, slo, shi)


def _qk(q, k):
    return lax.dot_general(q, k, (((1,), (1,)), ((), ())), preferred_element_type=F32)


def _attn_kernel(q_ref, k_ref, v_ref, g_ref, o_ref, *, tq):
    r = lax.broadcasted_iota(jnp.int32, (tq, tq), 0) // CHUNK
    c = lax.broadcasted_iota(jnp.int32, (tq, tq), 1) // CHUNK
    diag_visible = c <= r
    g = g_ref[...]
    for t0 in range(0, SEQ, tq):
        q = q_ref[t0:t0 + tq, :]
        s_d = jnp.where(diag_visible, _qk(q, k_ref[t0:t0 + tq, :]), NEG_INF)
        m = jnp.max(s_d, axis=-1, keepdims=True)
        if t0:
            s_p = _qk(q, k_ref[0:t0, :])
            m = jnp.maximum(m, jnp.max(s_p, axis=-1, keepdims=True))
        p_d = jnp.exp(s_d - m)
        l = jnp.sum(p_d, axis=-1, keepdims=True)
        acc = _dot(p_d.astype(BF16), v_ref[t0:t0 + tq, :])
        if t0:
            p_p = jnp.exp(s_p - m)
            l = l + jnp.sum(p_p, axis=-1, keepdims=True)
            acc = acc + _dot(p_p.astype(BF16), v_ref[0:t0, :])
        o_ref[t0:t0 + tq, :] = _rms(acc / l, g).astype(BF16)


def _attn_call(q, k, v, g, *, tq=512):
    return pl.pallas_call(
        functools.partial(_attn_kernel, tq=tq),
        grid=(BATCH, N_HEADS),
        in_specs=[pl.BlockSpec((SEQ, QK_PAD), lambda b, h: (b, h)),
                  pl.BlockSpec((SEQ, QK_PAD), lambda b, h: (b, h)),
                  pl.BlockSpec((SEQ, V_DIM), lambda b, h: (b, h)),
                  pl.BlockSpec((1, V_DIM), lambda b, h: (0, h))],
        out_specs=pl.BlockSpec((SEQ, V_DIM), lambda b, h: (b, h)),
        out_shape=jax.ShapeDtypeStruct((TOKENS, ATTN_WIDTH), BF16),
        compiler_params=pltpu.CompilerParams(
            dimension_semantics=("arbitrary", "arbitrary"),
            vmem_limit_bytes=V7X_VMEM_LIMIT_BYTES),
        name="attn",
    )(q, k, v, g)


def _mix_out_kernel(x_ref, yc_ref, ya_ref, wo_ref, gpost_ref, gmlp_ref, x1_ref, h2_ref):
    y = _dot(yc_ref[...], wo_ref[0:CONV_WIDTH, :]) + _dot(ya_ref[...], wo_ref[CONV_WIDTH:, :])
    x1 = x_ref[...] + _rms(y, gpost_ref[...])
    x1_ref[...] = x1
    h2_ref[...] = _rms(x1, gmlp_ref[...]).astype(BF16)


def _mix_out_call(x, yc, ya, wo, gpost, gmlp, *, tm=512):
    row = lambda w: pl.BlockSpec((tm, w), lambda i: (i, 0))
    return pl.pallas_call(
        _mix_out_kernel,
        grid=(TOKENS // tm,),
        in_specs=[row(D_MODEL), row(CONV_WIDTH), row(ATTN_WIDTH), _const_spec(wo.shape),
                  _const_spec((1, D_MODEL)), _const_spec((1, D_MODEL))],
        out_specs=[row(D_MODEL), row(D_MODEL)],
        out_shape=[jax.ShapeDtypeStruct((TOKENS, D_MODEL), F32),
                   jax.ShapeDtypeStruct((TOKENS, D_MODEL), BF16)],
        compiler_params=pltpu.CompilerParams(
            dimension_semantics=("arbitrary",), vmem_limit_bytes=V7X_VMEM_LIMIT_BYTES),
        name="mix_out",
    )(x, yc, ya, wo, gpost, gmlp)


def _mlp_kernel(h2_ref, wup_ref, wdn_ref, x1_ref, g_ref, o_ref, acc_ref):
    j = pl.program_id(1)

    @pl.when(j == 0)
    def _():
        acc_ref[...] = jnp.zeros_like(acc_ref)

    a = jnp.maximum(_dot(h2_ref[...], wup_ref[...]), 0.0)
    acc_ref[...] += _dot((a * a).astype(BF16), wdn_ref[...])

    @pl.when(j == pl.num_programs(1) - 1)
    def _():
        o_ref[...] = x1_ref[...] + _rms(acc_ref[...], g_ref[...])


def _mlp_call(h2, wup, wdn, x1, g, *, tm=512, tf=1024):
    return pl.pallas_call(
        _mlp_kernel,
        grid=(TOKENS // tm, D_FF // tf),
        in_specs=[pl.BlockSpec((tm, D_MODEL), lambda i, j: (i, 0)),
                  pl.BlockSpec((D_MODEL, tf), lambda i, j: (0, j)),
                  pl.BlockSpec((tf, D_MODEL), lambda i, j: (j, 0)),
                  pl.BlockSpec((tm, D_MODEL), lambda i, j: (i, 0)),
                  _const_spec((1, D_MODEL))],
        out_specs=pl.BlockSpec((tm, D_MODEL), lambda i, j: (i, 0)),
        out_shape=jax.ShapeDtypeStruct((TOKENS, D_MODEL), F32),
        scratch_shapes=[pltpu.VMEM((tm, D_MODEL), F32)],
        compiler_params=pltpu.CompilerParams(
            dimension_semantics=("arbitrary", "arbitrary"), vmem_limit_bytes=V7X_VMEM_LIMIT_BYTES),
        name="mlp",
    )(h2, wup, wdn, x1, g)


def _rope_tables():
    pos = np.arange(SEQ, dtype=np.float32)[:, None]
    inv_freq = np.power(np.float32(ROPE_THETA),
                        -np.arange(0, QK_ROPE, 2, dtype=np.float32) / np.float32(QK_ROPE))
    ang = (pos * inv_freq[None, :]).astype(np.float32)
    cos, sin = np.cos(ang).astype(np.float32), np.sin(ang).astype(np.float32)
    zero = np.zeros_like(sin)
    cos4 = np.concatenate([cos, cos, cos, cos], axis=1)
    sin_lo = np.concatenate([-sin, zero, -sin, zero], axis=1)
    sin_hi = np.concatenate([zero, sin, zero, sin], axis=1)
    return jnp.asarray(cos4), jnp.asarray(sin_lo), jnp.asarray(sin_hi)


def kernel(x, pre_mix_g, w_in, conv_w, q_norm_g, w_uq, kv_norm_g, w_ukv, conv_out_g, attn_out_g,
           w_o, post_mix_g, pre_mlp_g, w_up, w_down, post_mlp_g):
    cos4, sin_lo, sin_hi = _rope_tables()
    x2d = x.reshape(TOKENS, D_MODEL)
    for l in range(w_in.shape[0]):
        win = jnp.concatenate([w_in[l], w_in[l][:, _KR0:_KR0 + QK_ROPE]], axis=1).astype(BF16)
        wq3 = w_uq[l].reshape(Q_RANK, N_HEADS, QK_NOPE + QK_ROPE)
        wuq = jnp.concatenate([wq3[:, :, :QK_NOPE].reshape(Q_RANK, N_HEADS * QK_NOPE),
                               wq3[:, :, QK_NOPE:].reshape(Q_RANK, N_HEADS * QK_ROPE)],
                              axis=1).astype(BF16)
        wkv4 = w_ukv[l].reshape(KV_RANK, N_HEADS, 2, V_DIM)
        wukv = jnp.transpose(wkv4, (0, 2, 1, 3)).reshape(KV_RANK, 2 * N_HEADS * V_DIM).astype(BF16)
        convw = jnp.pad(conv_w[l], ((0, SUBLANES - CONV_K), (0, 0)))
        row = lambda g: g[l].reshape(1, -1)

        yc, q, k, v = _mix_in_call(x2d, row(pre_mix_g), win, convw, row(q_norm_g), wuq,
                                   row(kv_norm_g), wukv, row(conv_out_g), cos4, sin_lo, sin_hi)
        ya = _attn_call(q, k, v, row(attn_out_g))
        x1, h2 = _mix_out_call(x2d, yc, ya, w_o[l].astype(BF16), row(post_mix_g), row(pre_mlp_g))
        x2d = _mlp_call(h2, w_up[l].astype(BF16), w_down[l].astype(BF16), x1, row(post_mlp_g))
    return x2d.reshape(BATCH, SEQ, D_MODEL)
```

```python
import functools

import numpy as np
import jax
import jax.numpy as jnp
from jax import lax
from jax.experimental import pallas as pl
from jax.experimental.pallas import tpu as pltpu

D_MODEL = 2048
BATCH = 4
SEQ = 2048
TOKENS = BATCH * SEQ
CHUNK = 64
CONV_WIDTH = D_MODEL // 2
CONV_GROUPS = 8
CONV_K = 3
V_DIM = 128
N_HEADS = 8
QK_NOPE = 128
QK_ROPE = 64
Q_RANK = 768
KV_RANK = 512
ATTN_WIDTH = N_HEADS * V_DIM
D_FF = 4 * D_MODEL
ROPE_THETA = 10000.0
EPS = 1e-6
NEG_INF = -1e30
SM_SCALE = (QK_NOPE + QK_ROPE) ** -0.5

LANES = 128
SUBLANES = 8
QK_PAD = 2 * LANES
V7X_VMEM_LIMIT_BYTES = 60000 * 1024

_U0, _B0, _C0 = 0, CONV_WIDTH, 2 * CONV_WIDTH
_CQ0 = 3 * CONV_WIDTH
_CKV0 = _CQ0 + Q_RANK
_KR0 = _CKV0 + KV_RANK
IN_WIDTH = _KR0 + QK_ROPE

BF16 = jnp.bfloat16
F32 = jnp.float32


def _rms(x, g):
    ms = jnp.mean(x * x, axis=-1, keepdims=True)
    return x * lax.rsqrt(ms + EPS) * g


def _dot(a, b):
    return jnp.dot(a, b, preferred_element_type=F32)


def _rope_pairs(v, cos, sin_lo, sin_hi):
    return v * cos + pltpu.roll(v, LANES - 32, 1) * sin_lo + pltpu.roll(v, 32, 1) * sin_hi


def _mix_in_kernel(x_ref, gpre_ref, win_ref, wkr_ref, convw_ref, qg_ref, wuq_ref, kvg_ref, wukv_ref,
                   convg_ref, cos_ref, slo_ref, shi_ref,
                   yconv_ref, q_ref, k_ref, v_ref, h_scr, cu_scr, *, tm, tiles_per_seq):
    i = pl.program_id(0)

    @pl.when(i % tiles_per_seq == 0)
    def _():
        cu_scr[0:SUBLANES, :] = jnp.zeros((SUBLANES, CONV_WIDTH), F32)

    h_scr[...] = _rms(x_ref[...], gpre_ref[...]).astype(BF16)
    h = h_scr[...]

    cw = 2 * LANES
    for c in range(0, CONV_WIDTH, cw):
        u = _dot(h, win_ref[:, _U0 + c:_U0 + c + cw])
        gb = _dot(h, win_ref[:, _B0 + c:_B0 + c + cw])
        gc = _dot(h, win_ref[:, _C0 + c:_C0 + c + cw])
        cu_scr[SUBLANES:SUBLANES + tm, c:c + cw] = gc * u
        cu0 = cu_scr[SUBLANES:SUBLANES + tm, c:c + cw]
        cu1 = cu_scr[SUBLANES - 1:SUBLANES - 1 + tm, c:c + cw]
        cu2 = cu_scr[SUBLANES - 2:SUBLANES - 2 + tm, c:c + cw]
        w = convw_ref[:, c:c + cw]
        y = gb * (w[0:1, :] * cu2 + w[1:2, :] * cu1 + w[2:3, :] * cu0)
        for g in range(0, cw, LANES):
            yconv_ref[:, c + g:c + g + LANES] = _rms(
                y[:, g:g + LANES], convg_ref[:, c + g:c + g + LANES]).astype(BF16)
    cu_scr[SUBLANES - 2:SUBLANES, :] = cu_scr[SUBLANES + tm - 2:SUBLANES + tm, :]

    cos, slo, shi = cos_ref[...], slo_ref[...], shi_ref[...]
    lane = lax.broadcasted_iota(jnp.int32, (tm, LANES), 1)

    cqn = _rms(_dot(h, win_ref[:, _CQ0:_CQ0 + Q_RANK]), qg_ref[...]).astype(BF16)
    qn = _dot(cqn, wuq_ref[:, 0:N_HEADS * QK_NOPE]) * SM_SCALE
    qr = _dot(cqn, wuq_ref[:, N_HEADS * QK_NOPE:])
    for hd in range(N_HEADS):
        q_ref[:, hd * QK_PAD:hd * QK_PAD + LANES] = qn[:, hd * LANES:(hd + 1) * LANES].astype(BF16)
    for pr in range(N_HEADS // 2):
        roped = _rope_pairs(qr[:, pr * LANES:(pr + 1) * LANES], cos, slo, shi) * SM_SCALE
        lo = jnp.where(lane < QK_ROPE, roped, 0.0).astype(BF16)
        hi = jnp.where(lane >= QK_ROPE, roped, 0.0).astype(BF16)
        q_ref[:, (2 * pr) * QK_PAD + LANES:(2 * pr + 1) * QK_PAD] = lo
        q_ref[:, (2 * pr + 1) * QK_PAD + LANES:(2 * pr + 2) * QK_PAD] = hi

    ckvn = _rms(_dot(h, win_ref[:, _CKV0:_CKV0 + KV_RANK]), kvg_ref[...]).astype(BF16)
    kn = _dot(ckvn, wukv_ref[:, 0:N_HEADS * QK_NOPE])
    v_ref[...] = _dot(ckvn, wukv_ref[:, N_HEADS * QK_NOPE:]).astype(BF16)
    kr = _rope_pairs(_dot(h, wkr_ref[...]), cos, slo, shi).astype(BF16)
    for hd in range(N_HEADS):
        k_ref[:, hd * QK_PAD:hd * QK_PAD + LANES] = kn[:, hd * LANES:(hd + 1) * LANES].astype(BF16)
        k_ref[:, hd * QK_PAD + LANES:(hd + 1) * QK_PAD] = kr


def _const_spec(shape):
    return pl.BlockSpec(shape, lambda *_: (0,) * len(shape), pipeline_mode=pl.Buffered(1))


def _mix_in_call(x, gpre, win, wkr, convw, qg, wuq, kvg, wukv, convg, cos, slo, shi, *, tm=512):
    tiles_per_seq = SEQ // tm
    row = lambda w: pl.BlockSpec((tm, w), lambda i: (i, 0))
    pos = lambda: pl.BlockSpec((tm, LANES), lambda i: (i % tiles_per_seq, 0))
    return pl.pallas_call(
        functools.partial(_mix_in_kernel, tm=tm, tiles_per_seq=tiles_per_seq),
        grid=(TOKENS // tm,),
        in_specs=[row(D_MODEL), _const_spec((1, D_MODEL)), _const_spec((D_MODEL, IN_WIDTH)),
                  _const_spec((D_MODEL, LANES)),
                  _const_spec((SUBLANES, CONV_WIDTH)), _const_spec((1, Q_RANK)),
                  _const_spec(wuq.shape), _const_spec((1, KV_RANK)), _const_spec(wukv.shape),
                  _const_spec((1, CONV_WIDTH)), pos(), pos(), pos()],
        out_specs=[row(CONV_WIDTH), row(N_HEADS * QK_PAD), row(N_HEADS * QK_PAD), row(ATTN_WIDTH)],
        out_shape=[jax.ShapeDtypeStruct((TOKENS, CONV_WIDTH), BF16),
                   jax.ShapeDtypeStruct((TOKENS, N_HEADS * QK_PAD), BF16),
                   jax.ShapeDtypeStruct((TOKENS, N_HEADS * QK_PAD), BF16),
                   jax.ShapeDtypeStruct((TOKENS, ATTN_WIDTH), BF16)],
        scratch_shapes=[pltpu.VMEM((tm, D_MODEL), BF16),
                        pltpu.VMEM((tm + SUBLANES, CONV_WIDTH), F32)],
        compiler_params=pltpu.CompilerParams(
            dimension_semantics=("arbitrary",), vmem_limit_bytes=V7X_VMEM_LIMIT_BYTES),
        name="mix_in",
    )(x, gpre, win, wkr, convw, qg, wuq, kvg, wukv, convg, cos, slo, shi)


def _qk(q, k):
    return lax.dot_general(q, k, (((1,), (1,)), ((), ())), preferred_element_type=F32)


def _attn_kernel(q_ref, k_ref, v_ref, g_ref, wo_ref, wup_ref, wdn_ref,
                 o_ref, wo_bf_ref, wup_bf_ref, wdn_bf_ref, *, tq):
    wo_bf_ref[...] = wo_ref[...].astype(BF16)
    wup_bf_ref[...] = wup_ref[...].astype(BF16)
    wdn_bf_ref[...] = wdn_ref[...].astype(BF16)

    r = lax.broadcasted_iota(jnp.int32, (tq, tq), 0) // CHUNK
    c = lax.broadcasted_iota(jnp.int32, (tq, tq), 1) // CHUNK
    diag_visible = c <= r
    g = g_ref[...]
    for t0 in range(0, SEQ, tq):
        q = q_ref[t0:t0 + tq, :]
        s_d = jnp.where(diag_visible, _qk(q, k_ref[t0:t0 + tq, :]), NEG_INF)
        m = jnp.max(s_d, axis=-1, keepdims=True)
        if t0:
            s_p = _qk(q, k_ref[0:t0, :])
            m = jnp.maximum(m, jnp.max(s_p, axis=-1, keepdims=True))
        p_d = jnp.exp(s_d - m)
        l = jnp.sum(p_d, axis=-1, keepdims=True)
        acc = _dot(p_d.astype(BF16), v_ref[t0:t0 + tq, :])
        if t0:
            p_p = jnp.exp(s_p - m)
            l = l + jnp.sum(p_p, axis=-1, keepdims=True)
            acc = acc + _dot(p_p.astype(BF16), v_ref[0:t0, :])
        o_ref[t0:t0 + tq, :] = _rms(acc / l, g).astype(BF16)


def _attn_call(q, k, v, g, wo, wup, wdn, *, tq=512):
    steps = BATCH * N_HEADS

    def slab(w):
        return pl.BlockSpec((w.shape[0] // steps, w.shape[1]), lambda b, h: (b * N_HEADS + h, 0))

    return pl.pallas_call(
        functools.partial(_attn_kernel, tq=tq),
        grid=(BATCH, N_HEADS),
        in_specs=[pl.BlockSpec((SEQ, QK_PAD), lambda b, h: (b, h)),
                  pl.BlockSpec((SEQ, QK_PAD), lambda b, h: (b, h)),
                  pl.BlockSpec((SEQ, V_DIM), lambda b, h: (b, h)),
                  pl.BlockSpec((1, V_DIM), lambda b, h: (0, h)),
                  slab(wo), slab(wup), slab(wdn)],
        out_specs=[pl.BlockSpec((SEQ, V_DIM), lambda b, h: (b, h)), slab(wo), slab(wup), slab(wdn)],
        out_shape=[jax.ShapeDtypeStruct((TOKENS, ATTN_WIDTH), BF16),
                   jax.ShapeDtypeStruct(wo.shape, BF16),
                   jax.ShapeDtypeStruct(wup.shape, BF16),
                   jax.ShapeDtypeStruct(wdn.shape, BF16)],
        compiler_params=pltpu.CompilerParams(
            dimension_semantics=("arbitrary", "arbitrary"),
            vmem_limit_bytes=V7X_VMEM_LIMIT_BYTES),
        name="attn",
    )(q, k, v, g, wo, wup, wdn)


def _mix_out_kernel(x_ref, yc_ref, ya_ref, wo_ref, gpost_ref, gmlp_ref, x1_ref, h2_ref):
    y = _dot(yc_ref[...], wo_ref[0:CONV_WIDTH, :]) + _dot(ya_ref[...], wo_ref[CONV_WIDTH:, :])
    x1 = x_ref[...] + _rms(y, gpost_ref[...])
    x1_ref[...] = x1
    h2_ref[...] = _rms(x1, gmlp_ref[...]).astype(BF16)


def _mix_out_call(x, yc, ya, wo, gpost, gmlp, *, tm=512):
    row = lambda w: pl.BlockSpec((tm, w), lambda i: (i, 0))
    return pl.pallas_call(
        _mix_out_kernel,
        grid=(TOKENS // tm,),
        in_specs=[row(D_MODEL), row(CONV_WIDTH), row(ATTN_WIDTH), _const_spec(wo.shape),
                  _const_spec((1, D_MODEL)), _const_spec((1, D_MODEL))],
        out_specs=[row(D_MODEL), row(D_MODEL)],
        out_shape=[jax.ShapeDtypeStruct((TOKENS, D_MODEL), F32),
                   jax.ShapeDtypeStruct((TOKENS, D_MODEL), BF16)],
        compiler_params=pltpu.CompilerParams(
            dimension_semantics=("arbitrary",), vmem_limit_bytes=V7X_VMEM_LIMIT_BYTES),
        name="mix_out",
    )(x, yc, ya, wo, gpost, gmlp)


def _mlp_kernel(h2_ref, wup_ref, wdn_ref, x1_ref, g_ref, o_ref, acc_ref):
    j = pl.program_id(1)

    @pl.when(j == 0)
    def _():
        acc_ref[...] = jnp.zeros_like(acc_ref)

    a = jnp.maximum(_dot(h2_ref[...], wup_ref[...]), 0.0)
    acc_ref[...] += _dot((a * a).astype(BF16), wdn_ref[...])

    @pl.when(j == pl.num_programs(1) - 1)
    def _():
        o_ref[...] = x1_ref[...] + _rms(acc_ref[...], g_ref[...])


def _mlp_call(h2, wup, wdn, x1, g, *, tm=512, tf=1024):
    return pl.pallas_call(
        _mlp_kernel,
        grid=(TOKENS // tm, D_FF // tf),
        in_specs=[pl.BlockSpec((tm, D_MODEL), lambda i, j: (i, 0)),
                  pl.BlockSpec((D_MODEL, tf), lambda i, j: (0, j)),
                  pl.BlockSpec((tf, D_MODEL), lambda i, j: (j, 0)),
                  pl.BlockSpec((tm, D_MODEL), lambda i, j: (i, 0)),
                  _const_spec((1, D_MODEL))],
        out_specs=pl.BlockSpec((tm, D_MODEL), lambda i, j: (i, 0)),
        out_shape=jax.ShapeDtypeStruct((TOKENS, D_MODEL), F32),
        scratch_shapes=[pltpu.VMEM((tm, D_MODEL), F32)],
        compiler_params=pltpu.CompilerParams(
            dimension_semantics=("arbitrary", "arbitrary"), vmem_limit_bytes=V7X_VMEM_LIMIT_BYTES),
        name="mlp",
    )(h2, wup, wdn, x1, g)


def _rope_tables():
    pos = np.arange(SEQ, dtype=np.float32)[:, None]
    inv_freq = np.power(np.float32(ROPE_THETA),
                        -np.arange(0, QK_ROPE, 2, dtype=np.float32) / np.float32(QK_ROPE))
    ang = (pos * inv_freq[None, :]).astype(np.float32)
    cos, sin = np.cos(ang).astype(np.float32), np.sin(ang).astype(np.float32)
    zero = np.zeros_like(sin)
    cos4 = np.concatenate([cos, cos, cos, cos], axis=1)
    sin_lo = np.concatenate([-sin, zero, -sin, zero], axis=1)
    sin_hi = np.concatenate([zero, sin, zero, sin], axis=1)
    return jnp.asarray(cos4), jnp.asarray(sin_lo), jnp.asarray(sin_hi)


def kernel(x, pre_mix_g, w_in, conv_w, q_norm_g, w_uq, kv_norm_g, w_ukv, conv_out_g, attn_out_g,
           w_o, post_mix_g, pre_mlp_g, w_up, w_down, post_mlp_g):
    cos4, sin_lo, sin_hi = _rope_tables()
    x2d = x.reshape(TOKENS, D_MODEL)
    for l in range(w_in.shape[0]):
        win = w_in[l].astype(BF16)
        wkr = jnp.tile(w_in[l][:, _KR0:_KR0 + QK_ROPE], (1, LANES // QK_ROPE)).astype(BF16)
        wq3 = w_uq[l].reshape(Q_RANK, N_HEADS, QK_NOPE + QK_ROPE)
        wuq = jnp.concatenate([wq3[:, :, :QK_NOPE].reshape(Q_RANK, N_HEADS * QK_NOPE),
                               wq3[:, :, QK_NOPE:].reshape(Q_RANK, N_HEADS * QK_ROPE)],
                              axis=1).astype(BF16)
        wkv4 = w_ukv[l].reshape(KV_RANK, N_HEADS, 2, V_DIM)
        wukv = jnp.transpose(wkv4, (0, 2, 1, 3)).reshape(KV_RANK, 2 * N_HEADS * V_DIM).astype(BF16)
        convw = jnp.pad(conv_w[l], ((0, SUBLANES - CONV_K), (0, 0)))
        row = lambda g: g[l].reshape(1, -1)

        yc, q, k, v = _mix_in_call(x2d, row(pre_mix_g), win, wkr, convw, row(q_norm_g), wuq,
                                   row(kv_norm_g), wukv, row(conv_out_g), cos4, sin_lo, sin_hi)
        ya, wo, wup, wdn = _attn_call(q, k, v, row(attn_out_g), w_o[l], w_up[l], w_down[l])
        x1, h2 = _mix_out_call(x2d, yc, ya, wo, row(post_mix_g), row(pre_mlp_g))
        x2d = _mlp_call(h2, wup, wdn, x1, row(post_mlp_g))
    return x2d.reshape(BATCH, SEQ, D_MODEL)
```

```python
import functools

import numpy as np
import jax
import jax.numpy as jnp
from jax import lax
from jax.experimental import pallas as pl
from jax.experimental.pallas import tpu as pltpu

D_MODEL = 2048
BATCH = 4
SEQ = 2048
TOKENS = BATCH * SEQ
CHUNK = 64
CONV_WIDTH = D_MODEL // 2
CONV_GROUPS = 8
CONV_K = 3
V_DIM = 128
N_HEADS = 8
QK_NOPE = 128
QK_ROPE = 64
Q_RANK = 768
KV_RANK = 512
ATTN_WIDTH = N_HEADS * V_DIM
D_FF = 4 * D_MODEL
ROPE_THETA = 10000.0
EPS = 1e-6
NEG_INF = -1e30
SM_SCALE = (QK_NOPE + QK_ROPE) ** -0.5
LOG2_E = 1.4426950408889634
Q_SCALE = SM_SCALE * LOG2_E

LANES = 128
SUBLANES = 8
QK_PAD = 2 * LANES
V7X_VMEM_LIMIT_BYTES = 60000 * 1024

_U0, _B0, _C0 = 0, CONV_WIDTH, 2 * CONV_WIDTH
_CQ0 = 3 * CONV_WIDTH
_CKV0 = _CQ0 + Q_RANK
_KR0 = _CKV0 + KV_RANK
IN_WIDTH = _KR0 + QK_ROPE

BF16 = jnp.bfloat16
F32 = jnp.float32


def _rms(x, g):
    ms = jnp.mean(x * x, axis=-1, keepdims=True)
    return x * lax.rsqrt(ms + EPS) * g


def _dot(a, b):
    return jnp.dot(a, b, preferred_element_type=F32)


def _rope_pairs(v, cos, sin_lo, sin_hi):
    return v * cos + pltpu.roll(v, LANES - 32, 1) * sin_lo + pltpu.roll(v, 32, 1) * sin_hi


def _mix_in_kernel(x_ref, gpre_ref, win_ref, wkr_ref, convw_ref, qg_ref, wuq_ref, kvg_ref, wukv_ref,
                   convg_ref, cos_ref, slo_ref, shi_ref,
                   yconv_ref, q_ref, k_ref, v_ref, h_scr, cu_scr, *, tm, tiles_per_seq):
    i = pl.program_id(0)

    @pl.when(i % tiles_per_seq == 0)
    def _():
        cu_scr[0:SUBLANES, :] = jnp.zeros((SUBLANES, CONV_WIDTH), F32)

    h_scr[...] = _rms(x_ref[...], gpre_ref[...]).astype(BF16)
    h = h_scr[...]

    cw = 2 * LANES
    for c in range(0, CONV_WIDTH, cw):
        u = _dot(h, win_ref[:, _U0 + c:_U0 + c + cw])
        gb = _dot(h, win_ref[:, _B0 + c:_B0 + c + cw])
        gc = _dot(h, win_ref[:, _C0 + c:_C0 + c + cw])
        cu_scr[SUBLANES:SUBLANES + tm, c:c + cw] = gc * u
        cu0 = cu_scr[SUBLANES:SUBLANES + tm, c:c + cw]
        cu1 = cu_scr[SUBLANES - 1:SUBLANES - 1 + tm, c:c + cw]
        cu2 = cu_scr[SUBLANES - 2:SUBLANES - 2 + tm, c:c + cw]
        w = convw_ref[:, c:c + cw]
        y = gb * (w[0:1, :] * cu2 + w[1:2, :] * cu1 + w[2:3, :] * cu0)
        for g in range(0, cw, LANES):
            yconv_ref[:, c + g:c + g + LANES] = _rms(
                y[:, g:g + LANES], convg_ref[:, c + g:c + g + LANES]).astype(BF16)
    cu_scr[SUBLANES - 2:SUBLANES, :] = cu_scr[SUBLANES + tm - 2:SUBLANES + tm, :]

    cos, slo, shi = cos_ref[...], slo_ref[...], shi_ref[...]
    lane = lax.broadcasted_iota(jnp.int32, (tm, LANES), 1)

    cqn = _rms(_dot(h, win_ref[:, _CQ0:_CQ0 + Q_RANK]), qg_ref[...]).astype(BF16)
    qn = _dot(cqn, wuq_ref[:, 0:N_HEADS * QK_NOPE]) * Q_SCALE
    qr = _dot(cqn, wuq_ref[:, N_HEADS * QK_NOPE:])
    for hd in range(N_HEADS):
        q_ref[:, hd * QK_PAD:hd * QK_PAD + LANES] = qn[:, hd * LANES:(hd + 1) * LANES].astype(BF16)
    for pr in range(N_HEADS // 2):
        roped = _rope_pairs(qr[:, pr * LANES:(pr + 1) * LANES], cos, slo, shi) * Q_SCALE
        lo = jnp.where(lane < QK_ROPE, roped, 0.0).astype(BF16)
        hi = jnp.where(lane >= QK_ROPE, roped, 0.0).astype(BF16)
        q_ref[:, (2 * pr) * QK_PAD + LANES:(2 * pr + 1) * QK_PAD] = lo
        q_ref[:, (2 * pr + 1) * QK_PAD + LANES:(2 * pr + 2) * QK_PAD] = hi

    ckvn = _rms(_dot(h, win_ref[:, _CKV0:_CKV0 + KV_RANK]), kvg_ref[...]).astype(BF16)
    kn = _dot(ckvn, wukv_ref[:, 0:N_HEADS * QK_NOPE])
    v_ref[...] = _dot(ckvn, wukv_ref[:, N_HEADS * QK_NOPE:]).astype(BF16)
    kr = _rope_pairs(_dot(h, wkr_ref[...]), cos, slo, shi).astype(BF16)
    for hd in range(N_HEADS):
        k_ref[:, hd * QK_PAD:hd * QK_PAD + LANES] = kn[:, hd * LANES:(hd + 1) * LANES].astype(BF16)
        k_ref[:, hd * QK_PAD + LANES:(hd + 1) * QK_PAD] = kr


def _const_spec(shape):
    return pl.BlockSpec(shape, lambda *_: (0,) * len(shape), pipeline_mode=pl.Buffered(1))


def _mix_in_call(x, gpre, win, wkr, convw, qg, wuq, kvg, wukv, convg, cos, slo, shi, *, tm=512):
    tiles_per_seq = SEQ // tm
    row = lambda w: pl.BlockSpec((tm, w), lambda i: (i, 0))
    pos = lambda: pl.BlockSpec((tm, LANES), lambda i: (i % tiles_per_seq, 0))
    return pl.pallas_call(
        functools.partial(_mix_in_kernel, tm=tm, tiles_per_seq=tiles_per_seq),
        grid=(TOKENS // tm,),
        in_specs=[row(D_MODEL), _const_spec((1, D_MODEL)), _const_spec((D_MODEL, IN_WIDTH)),
                  _const_spec((D_MODEL, LANES)),
                  _const_spec((SUBLANES, CONV_WIDTH)), _const_spec((1, Q_RANK)),
                  _const_spec(wuq.shape), _const_spec((1, KV_RANK)), _const_spec(wukv.shape),
                  _const_spec((1, CONV_WIDTH)), pos(), pos(), pos()],
        out_specs=[row(CONV_WIDTH), row(N_HEADS * QK_PAD), row(N_HEADS * QK_PAD), row(ATTN_WIDTH)],
        out_shape=[jax.ShapeDtypeStruct((TOKENS, CONV_WIDTH), BF16),
                   jax.ShapeDtypeStruct((TOKENS, N_HEADS * QK_PAD), BF16),
                   jax.ShapeDtypeStruct((TOKENS, N_HEADS * QK_PAD), BF16),
                   jax.ShapeDtypeStruct((TOKENS, ATTN_WIDTH), BF16)],
        scratch_shapes=[pltpu.VMEM((tm, D_MODEL), BF16),
                        pltpu.VMEM((tm + SUBLANES, CONV_WIDTH), F32)],
        compiler_params=pltpu.CompilerParams(
            dimension_semantics=("arbitrary",), vmem_limit_bytes=V7X_VMEM_LIMIT_BYTES),
        name="mix_in",
    )(x, gpre, win, wkr, convw, qg, wuq, kvg, wukv, convg, cos, slo, shi)


def _qk(q, k):
    return lax.dot_general(q, k, (((1,), (1,)), ((), ())), preferred_element_type=F32)


def _attn_kernel(q_ref, k_ref, v_ref, g_ref, wo_ref, wup_ref, wdn_ref,
                 o_ref, wo_bf_ref, wup_bf_ref, wdn_bf_ref, *, tq, heads):
    wo_bf_ref[...] = wo_ref[...].astype(BF16)
    wup_bf_ref[...] = wup_ref[...].astype(BF16)
    wdn_bf_ref[...] = wdn_ref[...].astype(BF16)

    r = lax.broadcasted_iota(jnp.int32, (tq, tq), 0) // CHUNK
    c = lax.broadcasted_iota(jnp.int32, (tq, tq), 1) // CHUNK
    diag_visible = c <= r
    for t0 in reversed(range(0, SEQ, tq)):
        for hd in range(heads):
            qk_cols = slice(hd * QK_PAD, (hd + 1) * QK_PAD)
            v_cols = slice(hd * V_DIM, (hd + 1) * V_DIM)
            q = q_ref[t0:t0 + tq, qk_cols]
            s_d = jnp.where(diag_visible, _qk(q, k_ref[t0:t0 + tq, qk_cols]), NEG_INF)
            m = jnp.max(s_d, axis=-1, keepdims=True)
            if t0:
                s_p = _qk(q, k_ref[0:t0, qk_cols])
                m = jnp.maximum(m, jnp.max(s_p, axis=-1, keepdims=True))
            p_d = jnp.exp2(s_d - m)
            l = jnp.sum(p_d, axis=-1, keepdims=True)
            acc = _dot(p_d.astype(BF16), v_ref[t0:t0 + tq, v_cols])
            if t0:
                p_p = jnp.exp2(s_p - m)
                l = l + jnp.sum(p_p, axis=-1, keepdims=True)
                acc = acc + _dot(p_p.astype(BF16), v_ref[0:t0, v_cols])
            o_ref[t0:t0 + tq, v_cols] = _rms(acc / l, g_ref[:, v_cols]).astype(BF16)


def _attn_call(q, k, v, g, wo, wup, wdn, *, tq=512, heads=2):
    hsteps = N_HEADS // heads
    steps = BATCH * hsteps

    def slab(w):
        return pl.BlockSpec((w.shape[0] // steps, w.shape[1]), lambda b, h: (b * hsteps + h, 0))

    def cols(width):
        return pl.BlockSpec((SEQ, heads * width), lambda b, h: (b, h))

    return pl.pallas_call(
        functools.partial(_attn_kernel, tq=tq, heads=heads),
        grid=(BATCH, hsteps),
        in_specs=[cols(QK_PAD), cols(QK_PAD), cols(V_DIM),
                  pl.BlockSpec((1, heads * V_DIM), lambda b, h: (0, h)),
                  slab(wo), slab(wup), slab(wdn)],
        out_specs=[cols(V_DIM), slab(wo), slab(wup), slab(wdn)],
        out_shape=[jax.ShapeDtypeStruct((TOKENS, ATTN_WIDTH), BF16),
                   jax.ShapeDtypeStruct(wo.shape, BF16),
                   jax.ShapeDtypeStruct(wup.shape, BF16),
                   jax.ShapeDtypeStruct(wdn.shape, BF16)],
        compiler_params=pltpu.CompilerParams(
            dimension_semantics=("arbitrary", "arbitrary"),
            vmem_limit_bytes=V7X_VMEM_LIMIT_BYTES),
        name="attn",
    )(q, k, v, g, wo, wup, wdn)


def _mix_out_kernel(x_ref, yc_ref, ya_ref, wo_ref, gpost_ref, gmlp_ref, x1_ref, h2_ref):
    y = _dot(yc_ref[...], wo_ref[0:CONV_WIDTH, :]) + _dot(ya_ref[...], wo_ref[CONV_WIDTH:, :])
    x1 = x_ref[...] + _rms(y, gpost_ref[...])
    x1_ref[...] = x1
    h2_ref[...] = _rms(x1, gmlp_ref[...]).astype(BF16)


def _mix_out_call(x, yc, ya, wo, gpost, gmlp, *, tm=512):
    row = lambda w: pl.BlockSpec((tm, w), lambda i: (i, 0))
    return pl.pallas_call(
        _mix_out_kernel,
        grid=(TOKENS // tm,),
        in_specs=[row(D_MODEL), row(CONV_WIDTH), row(ATTN_WIDTH), _const_spec(wo.shape),
                  _const_spec((1, D_MODEL)), _const_spec((1, D_MODEL))],
        out_specs=[row(D_MODEL), row(D_MODEL)],
        out_shape=[jax.ShapeDtypeStruct((TOKENS, D_MODEL), F32),
                   jax.ShapeDtypeStruct((TOKENS, D_MODEL), BF16)],
        compiler_params=pltpu.CompilerParams(
            dimension_semantics=("arbitrary",), vmem_limit_bytes=V7X_VMEM_LIMIT_BYTES),
        name="mix_out",
    )(x, yc, ya, wo, gpost, gmlp)


def _mlp_kernel(h2_ref, wup_ref, wdn_ref, x1_ref, g_ref, o_ref, acc_ref):
    j = pl.program_id(1)

    @pl.when(j == 0)
    def _():
        acc_ref[...] = jnp.zeros_like(acc_ref)

    a = jnp.maximum(_dot(h2_ref[...], wup_ref[...]), 0.0)
    acc_ref[...] += _dot((a * a).astype(BF16), wdn_ref[...])

    @pl.when(j == pl.num_programs(1) - 1)
    def _():
        o_ref[...] = x1_ref[...] + _rms(acc_ref[...], g_ref[...])


def _mlp_call(h2, wup, wdn, x1, g, *, tm=512, tf=1024):
    return pl.pallas_call(
        _mlp_kernel,
        grid=(TOKENS // tm, D_FF // tf),
        in_specs=[pl.BlockSpec((tm, D_MODEL), lambda i, j: (i, 0)),
                  pl.BlockSpec((D_MODEL, tf), lambda i, j: (0, j)),
                  pl.BlockSpec((tf, D_MODEL), lambda i, j: (j, 0)),
                  pl.BlockSpec((tm, D_MODEL), lambda i, j: (i, 0)),
                  _const_spec((1, D_MODEL))],
        out_specs=pl.BlockSpec((tm, D_MODEL), lambda i, j: (i, 0)),
        out_shape=jax.ShapeDtypeStruct((TOKENS, D_MODEL), F32),
        scratch_shapes=[pltpu.VMEM((tm, D_MODEL), F32)],
        compiler_params=pltpu.CompilerParams(
            dimension_semantics=("arbitrary", "arbitrary"), vmem_limit_bytes=V7X_VMEM_LIMIT_BYTES),
        name="mlp",
    )(h2, wup, wdn, x1, g)


def _rope_tables():
    pos = np.arange(SEQ, dtype=np.float32)[:, None]
    inv_freq = np.power(np.float32(ROPE_THETA),
                        -np.arange(0, QK_ROPE, 2, dtype=np.float32) / np.float32(QK_ROPE))
    ang = (pos * inv_freq[None, :]).astype(np.float32)
    cos, sin = np.cos(ang).astype(np.float32), np.sin(ang).astype(np.float32)
    zero = np.zeros_like(sin)
    cos4 = np.concatenate([cos, cos, cos, cos], axis=1)
    sin_lo = np.concatenate([-sin, zero, -sin, zero], axis=1)
    sin_hi = np.concatenate([zero, sin, zero, sin], axis=1)
    return jnp.asarray(cos4), jnp.asarray(sin_lo), jnp.asarray(sin_hi)


def kernel(x, pre_mix_g, w_in, conv_w, q_norm_g, w_uq, kv_norm_g, w_ukv, conv_out_g, attn_out_g,
           w_o, post_mix_g, pre_mlp_g, w_up, w_down, post_mlp_g):
    cos4, sin_lo, sin_hi = _rope_tables()
    x2d = x.reshape(TOKENS, D_MODEL)
    for l in range(w_in.shape[0]):
        win = w_in[l].astype(BF16)
        wkr = jnp.tile(w_in[l][:, _KR0:_KR0 + QK_ROPE], (1, LANES // QK_ROPE)).astype(BF16)
        wq3 = w_uq[l].reshape(Q_RANK, N_HEADS, QK_NOPE + QK_ROPE)
        wuq = jnp.concatenate([wq3[:, :, :QK_NOPE].reshape(Q_RANK, N_HEADS * QK_NOPE),
                               wq3[:, :, QK_NOPE:].reshape(Q_RANK, N_HEADS * QK_ROPE)],
                              axis=1).astype(BF16)
        wkv4 = w_ukv[l].reshape(KV_RANK, N_HEADS, 2, V_DIM)
        wukv = jnp.transpose(wkv4, (0, 2, 1, 3)).reshape(KV_RANK, 2 * N_HEADS * V_DIM).astype(BF16)
        convw = jnp.pad(conv_w[l], ((0, SUBLANES - CONV_K), (0, 0)))
        row = lambda g: g[l].reshape(1, -1)

        yc, q, k, v = _mix_in_call(x2d, row(pre_mix_g), win, wkr, convw, row(q_norm_g), wuq,
                                   row(kv_norm_g), wukv, row(conv_out_g), cos4, sin_lo, sin_hi)
        ya, wo, wup, wdn = _attn_call(q, k, v, row(attn_out_g), w_o[l], w_up[l], w_down[l])
        x1, h2 = _mix_out_call(x2d, yc, ya, wo, row(post_mix_g), row(pre_mlp_g))
        x2d = _mlp_call(h2, wup, wdn, x1, row(post_mlp_g))
    return x2d.reshape(BATCH, SEQ, D_MODEL)
```

```python
import functools

import numpy as np
import jax
import jax.numpy as jnp
from jax import lax
from jax.experimental import pallas as pl
from jax.experimental.pallas import tpu as pltpu

D_MODEL = 2048
BATCH = 4
SEQ = 2048
TOKENS = BATCH * SEQ
CHUNK = 64
CONV_WIDTH = D_MODEL // 2
CONV_GROUPS = 8
CONV_K = 3
V_DIM = 128
N_HEADS = 8
QK_NOPE = 128
QK_ROPE = 64
Q_RANK = 768
KV_RANK = 512
ATTN_WIDTH = N_HEADS * V_DIM
D_FF = 4 * D_MODEL
ROPE_THETA = 10000.0
EPS = 1e-6
NEG_INF = -1e30
SM_SCALE = (QK_NOPE + QK_ROPE) ** -0.5
LOG2_E = 1.4426950408889634
Q_SCALE = SM_SCALE * LOG2_E

LANES = 128
SUBLANES = 8
QK_PAD = 2 * LANES
V7X_VMEM_LIMIT_BYTES = 60000 * 1024

_U0, _B0, _C0 = 0, CONV_WIDTH, 2 * CONV_WIDTH
_CQ0 = 3 * CONV_WIDTH
_CKV0 = _CQ0 + Q_RANK
_KR0 = _CKV0 + KV_RANK
IN_WIDTH = _KR0 + QK_ROPE

BF16 = jnp.bfloat16
F32 = jnp.float32


def _rms(x, g):
    ms = jnp.mean(x * x, axis=-1, keepdims=True)
    return x * lax.rsqrt(ms + EPS) * g


def _dot(a, b):
    return jnp.dot(a, b, preferred_element_type=F32)


def _rope_pairs(v, cos, sin_lo, sin_hi):
    return v * cos + pltpu.roll(v, LANES - 32, 1) * sin_lo + pltpu.roll(v, 32, 1) * sin_hi


def _mix_in_kernel(x_ref, gpre_ref, win_ref, wkr_ref, convw_ref, qg_ref, wuq_ref, kvg_ref, wukv_ref,
                   convg_ref, cos_ref, slo_ref, shi_ref,
                   yconv_ref, q_ref, k_ref, v_ref, h_scr, cu_scr, *, tm, tiles_per_seq):
    i = pl.program_id(0)

    @pl.when(i % tiles_per_seq == 0)
    def _():
        cu_scr[0:SUBLANES, :] = jnp.zeros((SUBLANES, CONV_WIDTH), F32)

    h_scr[...] = _rms(x_ref[...], gpre_ref[...]).astype(BF16)
    h = h_scr[...]

    cos, slo, shi = cos_ref[...], slo_ref[...], shi_ref[...]
    lane = lax.broadcasted_iota(jnp.int32, (tm, LANES), 1)
    cw = 2 * LANES

    def conv_dots(c):
        return (_dot(h, win_ref[:, _U0 + c:_U0 + c + cw]),
                _dot(h, win_ref[:, _B0 + c:_B0 + c + cw]),
                _dot(h, win_ref[:, _C0 + c:_C0 + c + cw]))

    def conv_finish(c, u, gb, gc):
        cu_scr[SUBLANES:SUBLANES + tm, c:c + cw] = gc * u
        cu0 = cu_scr[SUBLANES:SUBLANES + tm, c:c + cw]
        cu1 = cu_scr[SUBLANES - 1:SUBLANES - 1 + tm, c:c + cw]
        cu2 = cu_scr[SUBLANES - 2:SUBLANES - 2 + tm, c:c + cw]
        w = convw_ref[:, c:c + cw]
        y = gb * (w[0:1, :] * cu2 + w[1:2, :] * cu1 + w[2:3, :] * cu0)
        for g in range(0, cw, LANES):
            yconv_ref[:, c + g:c + g + LANES] = _rms(
                y[:, g:g + LANES], convg_ref[:, c + g:c + g + LANES]).astype(BF16)

    cq = _dot(h, win_ref[:, _CQ0:_CQ0 + Q_RANK])
    ckv = _dot(h, win_ref[:, _CKV0:_CKV0 + KV_RANK])
    kr_raw = _dot(h, wkr_ref[...])
    conv0 = conv_dots(0)

    cqn = _rms(cq, qg_ref[...]).astype(BF16)
    ckvn = _rms(ckv, kvg_ref[...]).astype(BF16)
    kr = _rope_pairs(kr_raw, cos, slo, shi).astype(BF16)

    qn = _dot(cqn, wuq_ref[:, 0:N_HEADS * QK_NOPE]) * Q_SCALE
    qr = _dot(cqn, wuq_ref[:, N_HEADS * QK_NOPE:])
    conv1 = conv_dots(cw)
    conv_finish(0, *conv0)
    for hd in range(N_HEADS):
        q_ref[:, hd * QK_PAD:hd * QK_PAD + LANES] = qn[:, hd * LANES:(hd + 1) * LANES].astype(BF16)
    for pr in range(N_HEADS // 2):
        roped = _rope_pairs(qr[:, pr * LANES:(pr + 1) * LANES], cos, slo, shi) * Q_SCALE
        lo = jnp.where(lane < QK_ROPE, roped, 0.0).astype(BF16)
        hi = jnp.where(lane >= QK_ROPE, roped, 0.0).astype(BF16)
        q_ref[:, (2 * pr) * QK_PAD + LANES:(2 * pr + 1) * QK_PAD] = lo
        q_ref[:, (2 * pr + 1) * QK_PAD + LANES:(2 * pr + 2) * QK_PAD] = hi

    kn = _dot(ckvn, wukv_ref[:, 0:N_HEADS * QK_NOPE])
    vv = _dot(ckvn, wukv_ref[:, N_HEADS * QK_NOPE:])
    conv2 = conv_dots(2 * cw)
    conv_finish(cw, *conv1)
    v_ref[...] = vv.astype(BF16)
    for hd in range(N_HEADS):
        k_ref[:, hd * QK_PAD:hd * QK_PAD + LANES] = kn[:, hd * LANES:(hd + 1) * LANES].astype(BF16)
        k_ref[:, hd * QK_PAD + LANES:(hd + 1) * QK_PAD] = kr

    conv3 = conv_dots(3 * cw)
    conv_finish(2 * cw, *conv2)
    conv_finish(3 * cw, *conv3)
    cu_scr[SUBLANES - 2:SUBLANES, :] = cu_scr[SUBLANES + tm - 2:SUBLANES + tm, :]


def _const_spec(shape):
    return pl.BlockSpec(shape, lambda *_: (0,) * len(shape), pipeline_mode=pl.Buffered(1))


def _mix_in_call(x, gpre, win, wkr, convw, qg, wuq, kvg, wukv, convg, cos, slo, shi, *, tm=512):
    tiles_per_seq = SEQ // tm
    row = lambda w: pl.BlockSpec((tm, w), lambda i: (i, 0))
    pos = lambda: pl.BlockSpec((tm, LANES), lambda i: (i % tiles_per_seq, 0))
    return pl.pallas_call(
        functools.partial(_mix_in_kernel, tm=tm, tiles_per_seq=tiles_per_seq),
        grid=(TOKENS // tm,),
        in_specs=[row(D_MODEL), _const_spec((1, D_MODEL)), _const_spec((D_MODEL, IN_WIDTH)),
                  _const_spec((D_MODEL, LANES)),
                  _const_spec((SUBLANES, CONV_WIDTH)), _const_spec((1, Q_RANK)),
                  _const_spec(wuq.shape), _const_spec((1, KV_RANK)), _const_spec(wukv.shape),
                  _const_spec((1, CONV_WIDTH)), pos(), pos(), pos()],
        out_specs=[row(CONV_WIDTH), row(N_HEADS * QK_PAD), row(N_HEADS * QK_PAD), row(ATTN_WIDTH)],
        out_shape=[jax.ShapeDtypeStruct((TOKENS, CONV_WIDTH), BF16),
                   jax.ShapeDtypeStruct((TOKENS, N_HEADS * QK_PAD), BF16),
                   jax.ShapeDtypeStruct((TOKENS, N_HEADS * QK_PAD), BF16),
                   jax.ShapeDtypeStruct((TOKENS, ATTN_WIDTH), BF16)],
        scratch_shapes=[pltpu.VMEM((tm, D_MODEL), BF16),
                        pltpu.VMEM((tm + SUBLANES, CONV_WIDTH), F32)],
        compiler_params=pltpu.CompilerParams(
            dimension_semantics=("arbitrary",), vmem_limit_bytes=V7X_VMEM_LIMIT_BYTES),
        name="mix_in",
    )(x, gpre, win, wkr, convw, qg, wuq, kvg, wukv, convg, cos, slo, shi)


def _qk(q, k):
    return lax.dot_general(q, k, (((1,), (1,)), ((), ())), preferred_element_type=F32)


def _attn_kernel(q_ref, k_ref, v_ref, g_ref, wo_ref, wup_ref, wdn_ref,
                 o_ref, wo_bf_ref, wup_bf_ref, wdn_bf_ref, *, tq, heads):
    wo_bf_ref[...] = wo_ref[...].astype(BF16)
    wup_bf_ref[...] = wup_ref[...].astype(BF16)
    wdn_bf_ref[...] = wdn_ref[...].astype(BF16)

    r = lax.broadcasted_iota(jnp.int32, (tq, tq), 0) // CHUNK
    c = lax.broadcasted_iota(jnp.int32, (tq, tq), 1) // CHUNK
    diag_visible = c <= r
    for t0 in reversed(range(0, SEQ, tq)):
        for hd in range(heads):
            qk_cols = slice(hd * QK_PAD, (hd + 1) * QK_PAD)
            v_cols = slice(hd * V_DIM, (hd + 1) * V_DIM)
            q = q_ref[t0:t0 + tq, qk_cols]
            s_d = jnp.where(diag_visible, _qk(q, k_ref[t0:t0 + tq, qk_cols]), NEG_INF)
            m = jnp.max(s_d, axis=-1, keepdims=True)
            if t0:
                s_p = _qk(q, k_ref[0:t0, qk_cols])
                m = jnp.maximum(m, jnp.max(s_p, axis=-1, keepdims=True))
            p_d = jnp.exp2(s_d - m)
            l = jnp.sum(p_d, axis=-1, keepdims=True)
            acc = _dot(p_d.astype(BF16), v_ref[t0:t0 + tq, v_cols])
            if t0:
                p_p = jnp.exp2(s_p - m)
                l = l + jnp.sum(p_p, axis=-1, keepdims=True)
                acc = acc + _dot(p_p.astype(BF16), v_ref[0:t0, v_cols])
            o_ref[t0:t0 + tq, v_cols] = _rms(acc / l, g_ref[:, v_cols]).astype(BF16)


def _attn_call(q, k, v, g, wo, wup, wdn, *, tq=512, heads=2):
    hsteps = N_HEADS // heads
    steps = BATCH * hsteps

    def slab(w):
        return pl.BlockSpec((w.shape[0] // steps, w.shape[1]), lambda b, h: (b * hsteps + h, 0))

    def cols(width):
        return pl.BlockSpec((SEQ, heads * width), lambda b, h: (b, h))

    return pl.pallas_call(
        functools.partial(_attn_kernel, tq=tq, heads=heads),
        grid=(BATCH, hsteps),
        in_specs=[cols(QK_PAD), cols(QK_PAD), cols(V_DIM),
                  pl.BlockSpec((1, heads * V_DIM), lambda b, h: (0, h)),
                  slab(wo), slab(wup), slab(wdn)],
        out_specs=[cols(V_DIM), slab(wo), slab(wup), slab(wdn)],
        out_shape=[jax.ShapeDtypeStruct((TOKENS, ATTN_WIDTH), BF16),
                   jax.ShapeDtypeStruct(wo.shape, BF16),
                   jax.ShapeDtypeStruct(wup.shape, BF16),
                   jax.ShapeDtypeStruct(wdn.shape, BF16)],
        compiler_params=pltpu.CompilerParams(
            dimension_semantics=("arbitrary", "arbitrary"),
            vmem_limit_bytes=V7X_VMEM_LIMIT_BYTES),
        name="attn",
    )(q, k, v, g, wo, wup, wdn)


def _mix_out_kernel(x_ref, yc_ref, ya_ref, wo_ref, gpost_ref, gmlp_ref, x1_ref, h2_ref):
    y = _dot(yc_ref[...], wo_ref[0:CONV_WIDTH, :]) + _dot(ya_ref[...], wo_ref[CONV_WIDTH:, :])
    x1 = x_ref[...] + _rms(y, gpost_ref[...])
    x1_ref[...] = x1
    h2_ref[...] = _rms(x1, gmlp_ref[...]).astype(BF16)


def _mix_out_call(x, yc, ya, wo, gpost, gmlp, *, tm=512):
    row = lambda w: pl.BlockSpec((tm, w), lambda i: (i, 0))
    return pl.pallas_call(
        _mix_out_kernel,
        grid=(TOKENS // tm,),
        in_specs=[row(D_MODEL), row(CONV_WIDTH), row(ATTN_WIDTH), _const_spec(wo.shape),
                  _const_spec((1, D_MODEL)), _const_spec((1, D_MODEL))],
        out_specs=[row(D_MODEL), row(D_MODEL)],
        out_shape=[jax.ShapeDtypeStruct((TOKENS, D_MODEL), F32),
                   jax.ShapeDtypeStruct((TOKENS, D_MODEL), BF16)],
        compiler_params=pltpu.CompilerParams(
            dimension_semantics=("arbitrary",), vmem_limit_bytes=V7X_VMEM_LIMIT_BYTES),
        name="mix_out",
    )(x, yc, ya, wo, gpost, gmlp)


def _mlp_kernel(h2_ref, wup_ref, wdn_ref, x1_ref, g_ref, o_ref, acc_ref):
    j = pl.program_id(1)

    @pl.when(j == 0)
    def _():
        acc_ref[...] = jnp.zeros_like(acc_ref)

    a = jnp.maximum(_dot(h2_ref[...], wup_ref[...]), 0.0)
    acc_ref[...] += _dot((a * a).astype(BF16), wdn_ref[...])

    @pl.when(j == pl.num_programs(1) - 1)
    def _():
        o_ref[...] = x1_ref[...] + _rms(acc_ref[...], g_ref[...])


def _mlp_call(h2, wup, wdn, x1, g, *, tm=512, tf=1024):
    return pl.pallas_call(
        _mlp_kernel,
        grid=(TOKENS // tm, D_FF // tf),
        in_specs=[pl.BlockSpec((tm, D_MODEL), lambda i, j: (i, 0)),
                  pl.BlockSpec((D_MODEL, tf), lambda i, j: (0, j)),
                  pl.BlockSpec((tf, D_MODEL), lambda i, j: (j, 0)),
                  pl.BlockSpec((tm, D_MODEL), lambda i, j: (i, 0)),
                  _const_spec((1, D_MODEL))],
        out_specs=pl.BlockSpec((tm, D_MODEL), lambda i, j: (i, 0)),
        out_shape=jax.ShapeDtypeStruct((TOKENS, D_MODEL), F32),
        scratch_shapes=[pltpu.VMEM((tm, D_MODEL), F32)],
        compiler_params=pltpu.CompilerParams(
            dimension_semantics=("arbitrary", "arbitrary"), vmem_limit_bytes=V7X_VMEM_LIMIT_BYTES),
        name="mlp",
    )(h2, wup, wdn, x1, g)


def _rope_tables():
    pos = np.arange(SEQ, dtype=np.float32)[:, None]
    inv_freq = np.power(np.float32(ROPE_THETA),
                        -np.arange(0, QK_ROPE, 2, dtype=np.float32) / np.float32(QK_ROPE))
    ang = (pos * inv_freq[None, :]).astype(np.float32)
    cos, sin = np.cos(ang).astype(np.float32), np.sin(ang).astype(np.float32)
    zero = np.zeros_like(sin)
    cos4 = np.concatenate([cos, cos, cos, cos], axis=1)
    sin_lo = np.concatenate([-sin, zero, -sin, zero], axis=1)
    sin_hi = np.concatenate([zero, sin, zero, sin], axis=1)
    return jnp.asarray(cos4), jnp.asarray(sin_lo), jnp.asarray(sin_hi)


def kernel(x, pre_mix_g, w_in, conv_w, q_norm_g, w_uq, kv_norm_g, w_ukv, conv_out_g, attn_out_g,
           w_o, post_mix_g, pre_mlp_g, w_up, w_down, post_mlp_g):
    cos4, sin_lo, sin_hi = _rope_tables()
    x2d = x.reshape(TOKENS, D_MODEL)
    for l in range(w_in.shape[0]):
        win = w_in[l].astype(BF16)
        wkr = jnp.tile(w_in[l][:, _KR0:_KR0 + QK_ROPE], (1, LANES // QK_ROPE)).astype(BF16)
        wq3 = w_uq[l].reshape(Q_RANK, N_HEADS, QK_NOPE + QK_ROPE)
        wuq = jnp.concatenate([wq3[:, :, :QK_NOPE].reshape(Q_RANK, N_HEADS * QK_NOPE),
                               wq3[:, :, QK_NOPE:].reshape(Q_RANK, N_HEADS * QK_ROPE)],
                              axis=1).astype(BF16)
        wkv4 = w_ukv[l].reshape(KV_RANK, N_HEADS, 2, V_DIM)
        wukv = jnp.transpose(wkv4, (0, 2, 1, 3)).reshape(KV_RANK, 2 * N_HEADS * V_DIM).astype(BF16)
        convw = jnp.pad(conv_w[l], ((0, SUBLANES - CONV_K), (0, 0)))
        row = lambda g: g[l].reshape(1, -1)

        yc, q, k, v = _mix_in_call(x2d, row(pre_mix_g), win, wkr, convw, row(q_norm_g), wuq,
                                   row(kv_norm_g), wukv, row(conv_out_g), cos4, sin_lo, sin_hi)
        ya, wo, wup, wdn = _attn_call(q, k, v, row(attn_out_g), w_o[l], w_up[l], w_down[l])
        x1, h2 = _mix_out_call(x2d, yc, ya, wo, row(post_mix_g), row(pre_mlp_g))
        x2d = _mlp_call(h2, wup, wdn, x1, row(post_mlp_g))
    return x2d.reshape(BATCH, SEQ, D_MODEL)
```
